```python
import math
import jax, jax.numpy as jnp
from jax import lax
import numpy as np

D_MODEL = 1024
BATCH = 4
SEQ = 4096
DEPTH = 2

BRANCH_W = D_MODEL // 2
H_A = 4
DV_A = BRANCH_W // H_A
DK_A = DV_A // 2
C_CONV = BRANCH_W
CONV_WIDTH = 31
H_R = 4
DK_R = BRANCH_W // H_R
DV_R = BRANCH_W // H_R
RET_CHUNK = 128
QBLK = 128
N_BRANCH = 3
D_FF = -(-8 * D_MODEL // (3 * 256)) * 256

COL_SIZES = (
    H_A * 2 * DK_A,
    H_A * 2 * DK_A,
    H_A * DV_A,
    2 * C_CONV,
    H_R * DK_R,
    H_R * DK_R,
    H_R * DV_R,
    H_R * DV_R,
    N_BRANCH * D_MODEL,
)
D_IN = sum(COL_SIZES)

kernel_name = "hybrid_diffattn_conformer_retention_gated"


def _rms(x, g, eps=1e-6):
    xf = x.astype(jnp.float32)
    y = xf * lax.rsqrt(jnp.mean(xf * xf, axis=-1, keepdims=True) + eps)
    return (y * g.astype(jnp.float32)).astype(x.dtype)


def _layer_norm(x, g, b, eps=1e-5):
    xf = x.astype(jnp.float32)
    mu = jnp.mean(xf, axis=-1, keepdims=True)
    xc = xf - mu
    y = xc * lax.rsqrt(jnp.mean(xc * xc, axis=-1, keepdims=True) + eps)
    return (y * g.astype(jnp.float32) + b.astype(jnp.float32)).astype(x.dtype)


def _diff_attention(q, k, v, lam):
    B, S = q.shape[0], q.shape[1]
    nblk = S // QBLK
    scale = DK_A ** -0.5
    slopes = 2.0 ** (-8.0 * (jnp.arange(H_A, dtype=jnp.float32) + 1.0) / H_A)
    kpos = jnp.arange(S)
    qb = q.reshape(B, nblk, QBLK, H_A, 2, DK_A).transpose(1, 0, 2, 3, 4, 5)

    def one_block(args):
        qi, i = args
        qpos = i * QBLK + jnp.arange(QBLK)
        dist = qpos[:, None] - kpos[None, :]
        s = jnp.einsum('bqhmd,bkhmd->bhmqk', qi, k,
                       preferred_element_type=jnp.float32) * scale
        s = s - slopes[:, None, None, None] * dist.astype(jnp.float32)
        s = jnp.where(dist >= 0, s, -jnp.inf)
        p = jax.nn.softmax(s, axis=-1)
        a = p[:, :, 0] - lam * p[:, :, 1]
        return jnp.einsum('bhqk,bkhv->bqhv', a.astype(v.dtype), v)

    o = lax.map(one_block, (qb, jnp.arange(nblk)))
    return o.transpose(1, 0, 2, 3, 4).reshape(B, S, H_A, DV_A)


def _conformer_conv(u, w_dw, b_dw, ln_g, ln_b):
    a, gt = jnp.split(u, 2, axis=-1)
    z = a * jax.nn.sigmoid(gt)
    z = lax.conv_general_dilated(
        z, w_dw[:, None, :], window_strides=(1,),
        padding=[(CONV_WIDTH - 1, 0)],
        dimension_numbers=('NWC', 'WIO', 'NWC'),
        feature_group_count=C_CONV) + b_dw
    z = _layer_norm(z, ln_g, ln_b)
    return jax.nn.silu(z)


def _retention(q, k, v):
    B, S = q.shape[0], q.shape[1]
    C = RET_CHUNK
    N = S // C
    log_g = jnp.log(1.0 - 2.0 ** (-5.0 - jnp.arange(H_R, dtype=jnp.float32)))
    k = k * (DK_R ** -0.5)
    qc = q.reshape(B, N, C, H_R, DK_R)
    kc = k.reshape(B, N, C, H_R, DK_R)
    vc = v.reshape(B, N, C, H_R, DV_R)
    idx = jnp.arange(C)
    diff = idx[:, None] - idx[None, :]
    inner_decay = jnp.where(diff >= 0,
                            jnp.exp(jnp.maximum(diff, 0).astype(jnp.float32)[None] * log_g[:, None, None]),
                            0.0)
    s = jnp.einsum('bnqhd,bnkhd->bnhqk', qc, kc,
                   preferred_element_type=jnp.float32) * inner_decay
    o_inner = jnp.einsum('bnhqk,bnkhv->bnqhv', s, vc.astype(jnp.float32))
    to_end = jnp.exp((C - 1 - idx).astype(jnp.float32)[:, None] * log_g[None, :])
    kv = jnp.einsum('bnkhd,bnkhv->bnhdv', kc.astype(jnp.float32) * to_end[:, :, None],
                    vc.astype(jnp.float32))
    chunk_decay = jnp.exp(C * log_g)[:, None, None]

    def step(R, kv_i):
        return chunk_decay * R + kv_i, R

    R0 = jnp.zeros((B, H_R, DK_R, DV_R), jnp.float32)
    _, R_prev = lax.scan(step, R0, kv.transpose(1, 0, 2, 3, 4))
    R_prev = R_prev.transpose(1, 0, 2, 3, 4)
    from_start = jnp.exp((idx + 1).astype(jnp.float32)[:, None] * log_g[None, :])
    o_cross = jnp.einsum('bnqhd,bnhdv->bnqhv',
                         qc.astype(jnp.float32) * from_start[:, :, None], R_prev)
    o = (o_inner + o_cross).reshape(B, S, H_R, DV_R)
    return o.astype(v.dtype)


def setup_inputs(seed: int = 0) -> dict:
    key = jax.random.key(seed)
    ks = jax.random.split(key, 20)
    f32 = jnp.float32
    n = lambda k, shape, s: (jax.random.normal(k, shape, f32) * s)
    return {
        "x": jax.random.normal(ks[0], (BATCH, SEQ, D_MODEL), f32),
        "norm1_g": 1.0 + n(ks[1], (DEPTH, D_MODEL), 0.02),
        "w_in": n(ks[2], (DEPTH, D_MODEL, D_IN), D_MODEL ** -0.5),
        "q_norm_g": 1.0 + n(ks[3], (DEPTH, DK_A), 0.02),
        "k_norm_g": 1.0 + n(ks[4], (DEPTH, DK_A), 0.02),
        "diff_lambda": n(ks[5], (DEPTH, 4, DK_A), 0.1),
        "attn_sub_g": 1.0 + n(ks[6], (DEPTH, DV_A), 0.02),
        "conv_w": n(ks[7], (DEPTH, CONV_WIDTH, C_CONV), CONV_WIDTH ** -0.5),
        "conv_b": n(ks[8], (DEPTH, C_CONV), 0.01),
        "conv_ln_g": 1.0 + n(ks[9], (DEPTH, C_CONV), 0.02),
        "conv_ln_b": n(ks[10], (DEPTH, C_CONV), 0.01),
        "ret_gn_g": 1.0 + n(ks[11], (DEPTH, DV_R), 0.02),
        "b_gate": n(ks[12], (DEPTH, N_BRANCH * D_MODEL), 0.01),
        "w_branch": n(ks[13], (DEPTH, N_BRANCH, BRANCH_W, D_MODEL), BRANCH_W ** -0.5),
        "w_out": n(ks[14], (DEPTH, D_MODEL, D_MODEL), D_MODEL ** -0.5),
        "norm2_g": 1.0 + n(ks[15], (DEPTH, D_MODEL), 0.02),
        "w_ffn_in": n(ks[16], (DEPTH, D_MODEL, 2 * D_FF), D_MODEL ** -0.5),
        "w_ffn_out": n(ks[17], (DEPTH, D_FF, D_MODEL), D_FF ** -0.5),
    }


def reference(x, norm1_g, w_in, q_norm_g, k_norm_g, diff_lambda, attn_sub_g,
              conv_w, conv_b, conv_ln_g, conv_ln_b, ret_gn_g, b_gate, w_branch,
              w_out, norm2_g, w_ffn_in, w_ffn_out):
    B, S = x.shape[0], x.shape[1]
    split_pts = list(np.cumsum(COL_SIZES)[:-1])
    for l in range(DEPTH):
        lam_init = 0.8 - 0.6 * math.exp(-0.3 * l)
        h = _rms(x, norm1_g[l])
        u = h @ w_in[l]
        aq, ak, av, cu, rq, rk, rv, rg, gl = jnp.split(u, split_pts, axis=-1)

        aq = _rms(aq.reshape(B, S, H_A, 2, DK_A), q_norm_g[l])
        ak = _rms(ak.reshape(B, S, H_A, 2, DK_A), k_norm_g[l])
        av = av.reshape(B, S, H_A, DV_A)
        lq1, lk1, lq2, lk2 = diff_lambda[l, 0], diff_lambda[l, 1], diff_lambda[l, 2], diff_lambda[l, 3]
        lam = (jnp.exp(jnp.sum(lq1.astype(jnp.float32) * lk1.astype(jnp.float32)))
               - jnp.exp(jnp.sum(lq2.astype(jnp.float32) * lk2.astype(jnp.float32)))
               + lam_init)
        ya = _diff_attention(aq, ak, av, lam)
        ya = (_rms(ya, attn_sub_g[l]) * (1.0 - lam_init)).reshape(B, S, BRANCH_W)

        yb = _conformer_conv(cu, conv_w[l], conv_b[l], conv_ln_g[l], conv_ln_b[l])

        yc = _retention(rq.reshape(B, S, H_R, DK_R), rk.reshape(B, S, H_R, DK_R),
                        rv.reshape(B, S, H_R, DV_R))
        yc = _rms(yc, ret_gn_g[l]).reshape(B, S, BRANCH_W) * jax.nn.silu(rg)

        br = jnp.stack([ya, yb, yc], axis=2)
        yproj = jnp.einsum('bsnc,ncd->bsnd', br, w_branch[l])
        gates = jax.nn.sigmoid(gl + b_gate[l]).reshape(B, S, N_BRANCH, D_MODEL)
        y = jnp.sum(gates * yproj, axis=2)
        x = x + y @ w_out[l]

        h2 = _rms(x, norm2_g[l])
        g, up = jnp.split(h2 @ w_ffn_in[l], 2, axis=-1)
        x = x + (jax.nn.silu(g) * up) @ w_ffn_out[l]
    return x
```

```python
import functools
import math

import jax
import jax.numpy as jnp
from jax import lax
from jax.experimental import pallas as pl
from jax.experimental.pallas import tpu as pltpu

F32 = jnp.float32
BF16 = jnp.bfloat16

D_MODEL = 1024
BRANCH_W = D_MODEL // 2
H_A = 4
DV_A = BRANCH_W // H_A
DK_A = DV_A // 2
C_CONV = BRANCH_W
CONV_WIDTH = 31
H_R = 4
DK_R = BRANCH_W // H_R
DV_R = BRANCH_W // H_R
RET_CHUNK = 128
N_BRANCH = 3
D_FF = -(-8 * D_MODEL // (3 * 256)) * 256
D_IN = 4 * BRANCH_W + 2 * C_CONV + 3 * BRANCH_W + N_BRANCH * D_MODEL

SEG = 512
U_COLS = D_IN - C_CONV
U_GATE_OFF = 8 * SEG
VMEM_LIMIT = 56 * 1024 * 1024

TM_PROJ = 512
BQ = 512
TC_CONV = 512
CONV_HALO = 32
CONV_ROWS = 64
FF_CHUNK = 512


def _cparams(sem):
    return pltpu.CompilerParams(dimension_semantics=sem,
                                vmem_limit_bytes=VMEM_LIMIT)


def _resident(shape):
    nd = len(shape)
    return pl.BlockSpec(shape, lambda *_: (0,) * nd,
                        pipeline_mode=pl.Buffered(1))


def _sigmoid(x):
    return 1.0 / (1.0 + jnp.exp(-x))


def _inproj_kernel(x_ref, g1_ref, w_ref, qg_ref, kg_ref, gsum_ref, bg_ref,
                   u_ref):
    xf = x_ref[...]
    ms = jnp.mean(xf * xf, axis=-1, keepdims=True)
    h = (xf * lax.rsqrt(ms + 1e-6) * g1_ref[...]).astype(BF16)

    def mm(c):
        return jnp.dot(h, w_ref[:, c * SEG:(c + 1) * SEG],
                       preferred_element_type=F32)

    def qk_norm(a, g):
        sq = a * a
        hi = sq.astype(BF16)
        lo = (sq - hi.astype(F32)).astype(BF16)
        ss = (jnp.dot(hi, gsum_ref[...], preferred_element_type=F32)
              + jnp.dot(lo, gsum_ref[...], preferred_element_type=F32))
        return a * lax.rsqrt(ss * (1.0 / DK_A) + 1e-6) * g

    u_ref[:, 0 * SEG:1 * SEG] = qk_norm(mm(0), qg_ref[...]).astype(BF16)
    u_ref[:, 1 * SEG:2 * SEG] = qk_norm(mm(1), kg_ref[...]).astype(BF16)
    u_ref[:, 2 * SEG:3 * SEG] = mm(2).astype(BF16)
    u_ref[:, 3 * SEG:4 * SEG] = (mm(3) * _sigmoid(mm(4))).astype(BF16)
    for c in (5, 6, 7):
        u_ref[:, (c - 1) * SEG:c * SEG] = mm(c).astype(BF16)
    rg = mm(8)
    u_ref[:, 7 * SEG:8 * SEG] = (rg * _sigmoid(rg)).astype(BF16)
    for c in range(9, 15):
        b = bg_ref[:, (c - 9) * SEG:(c - 8) * SEG]
        u_ref[:, (c - 1) * SEG:c * SEG] = _sigmoid(mm(c) + b).astype(BF16)


def _inproj(x2, g1, w, qg, kg, gsum, bg):
    T = x2.shape[0]
    return pl.pallas_call(
        _inproj_kernel,
        grid=(T // TM_PROJ,),
        in_specs=[
            pl.BlockSpec((TM_PROJ, D_MODEL), lambda i: (i, 0)),
            _resident((1, D_MODEL)),
            _resident((D_MODEL, D_IN)),
            _resident((1, SEG)),
            _resident((1, SEG)),
            _resident((SEG, SEG)),
            _resident((1, N_BRANCH * D_MODEL)),
        ],
        out_specs=pl.BlockSpec((TM_PROJ, U_COLS), lambda i: (i, 0)),
        out_shape=jax.ShapeDtypeStruct((T, U_COLS), BF16),
        compiler_params=_cparams(("arbitrary",)),
        name="inproj",
    )(x2, g1, w, qg, kg, gsum, bg)


def _attn_kernel(q_ref, k_ref, v_ref, dl_ref, sg_ref, o_ref,
                 m_ref, l_ref, acc_ref, *, lam_init):
    h = pl.program_id(1)
    i = pl.program_id(2)
    hf = (h + 1).astype(F32)
    slope = jnp.exp2(jnp.full((1, 1), -2.0, F32) * hf)

    q = q_ref[...]
    lane = lax.broadcasted_iota(jnp.int32, q.shape, 1)
    zero = jnp.zeros_like(q)
    qs = (jnp.where(lane < DK_A, q, zero), jnp.where(lane >= DK_A, q, zero))

    row = lax.broadcasted_iota(jnp.int32, (BQ, BQ), 0)
    col = lax.broadcasted_iota(jnp.int32, (BQ, BQ), 1)
    rel = (col - row).astype(F32) * slope
    rel_diag = jnp.where(col <= row, rel, -1e30)

    m_ref[...] = jnp.full(m_ref.shape, -1e30, F32)
    l_ref[...] = jnp.zeros(l_ref.shape, F32)
    acc_ref[...] = jnp.zeros(acc_ref.shape, F32)

    def step(j, bias):
        start = pl.multiple_of(j * BQ, BQ)
        kb = k_ref[pl.ds(start, BQ), :]
        vb = v_ref[pl.ds(start, BQ), :]
        c = slope * ((i - j) * BQ).astype(F32)
        for mi in range(2):
            s = lax.dot_general(qs[mi], kb, (((1,), (1,)), ((), ())),
                                preferred_element_type=F32) + bias
            m_old = m_ref[mi]
            m_new = jnp.maximum(m_old, jnp.max(s, axis=-1, keepdims=True) - c)
            p = jnp.exp(s - (m_new + c))
            alpha = jnp.exp(m_old - m_new)
            l_ref[mi] = alpha * l_ref[mi] + jnp.sum(p, axis=-1, keepdims=True)
            acc_ref[mi] = alpha * acc_ref[mi] + jnp.dot(
                p.astype(BF16), vb, preferred_element_type=F32)
            m_ref[mi] = m_new

    def body(j, carry):
        step(j, rel)
        return carry

    lax.fori_loop(0, i, body, 0)
    step(i, rel_diag)

    dl = dl_ref[...]
    lam = (jnp.exp(jnp.sum(dl[0:1] * dl[1:2], axis=-1, keepdims=True))
           - jnp.exp(jnp.sum(dl[2:3] * dl[3:4], axis=-1, keepdims=True))
           + lam_init)
    o = acc_ref[0] / l_ref[0] - lam * (acc_ref[1] / l_ref[1])
    ms = jnp.mean(o * o, axis=-1, keepdims=True)
    y = o * lax.rsqrt(ms + 1e-6) * sg_ref[...] * (1.0 - lam_init)
    o_ref[...] = y.astype(o_ref.dtype)


def _attention(u, dl, sg, lam_init, B, S):
    nq = S // BQ
    kern = functools.partial(_attn_kernel, lam_init=lam_init)
    return pl.pallas_call(
        kern,
        grid=(B, H_A, nq),
        in_specs=[
            pl.BlockSpec((BQ, DV_A), lambda b, h, i: (b * nq + i, h)),
            pl.BlockSpec((S, DV_A), lambda b, h, i: (b, H_A + h)),
            pl.BlockSpec((S, DV_A), lambda b, h, i: (b, 2 * H_A + h)),
            _resident((4, DK_A)),
            _resident((1, DV_A)),
        ],
        out_specs=pl.BlockSpec((BQ, DV_A), lambda b, h, i: (b * nq + i, h)),
        out_shape=jax.ShapeDtypeStruct((B * S, BRANCH_W), BF16),
        scratch_shapes=[
            pltpu.VMEM((2, BQ, 1), F32),
            pltpu.VMEM((2, BQ, 1), F32),
            pltpu.VMEM((2, BQ, DV_A), F32),
        ],
        compiler_params=_cparams(("arbitrary",) * 3),
        name="diff_attn",
    )(u, u, u, dl, sg)


def _conv_kernel(z_ref, w_ref, b_ref, lg_ref, lb_ref, o_ref, zbuf):
    s = pl.program_id(1)

    @pl.when(s == 0)
    def _():
        zbuf[0:CONV_HALO, :] = jnp.zeros((CONV_HALO, C_CONV), F32)

    zbuf[CONV_HALO:CONV_HALO + TC_CONV, :] = z_ref[...].astype(F32)

    off = CONV_HALO - (CONV_WIDTH - 1)
    for r in range(TC_CONV // CONV_ROWS):
        base = r * CONV_ROWS
        acc = jnp.broadcast_to(b_ref[...], (CONV_ROWS, C_CONV))
        for w in range(CONV_WIDTH):
            acc = acc + zbuf[base + off + w:base + off + w + CONV_ROWS, :] * w_ref[w:w + 1, :]
        mu = jnp.mean(acc, axis=-1, keepdims=True)
        xc = acc - mu
        var = jnp.mean(xc * xc, axis=-1, keepdims=True)
        y = xc * lax.rsqrt(var + 1e-5) * lg_ref[...] + lb_ref[...]
        o_ref[base:base + CONV_ROWS, :] = (y * _sigmoid(y)).astype(o_ref.dtype)

    zbuf[0:CONV_HALO, :] = zbuf[TC_CONV:TC_CONV + CONV_HALO, :]


def _conv(u, w, b, lg, lb, B, S):
    ns = S // TC_CONV
    return pl.pallas_call(
        _conv_kernel,
        grid=(B, ns),
        in_specs=[
            pl.BlockSpec((TC_CONV, C_CONV), lambda b, s: (b * ns + s, 3)),
            _resident((CONV_WIDTH, C_CONV)),
            _resident((1, C_CONV)),
            _resident((1, C_CONV)),
            _resident((1, C_CONV)),
        ],
        out_specs=pl.BlockSpec((TC_CONV, C_CONV), lambda b, s: (b * ns + s, 0)),
        out_shape=jax.ShapeDtypeStruct((B * S, BRANCH_W), BF16),
        scratch_shapes=[pltpu.VMEM((CONV_HALO + TC_CONV, C_CONV), F32)],
        compiler_params=_cparams(("arbitrary", "arbitrary")),
        name="conformer_conv",
    )(u, w, b, lg, lb)


def _ret_kernel(q_ref, k_ref, v_ref, g_ref, dm_ref, te_ref, fs_ref, cd_ref,
                gn_ref, o_ref, r_ref, *, n_chunks):
    C = RET_CHUNK
    r_ref[...] = jnp.zeros(r_ref.shape, F32)
    dmask = dm_ref[...]
    to_end = te_ref[...]
    from_start = fs_ref[...]
    cd = cd_ref[...]

    def body(n, carry):
        start = pl.multiple_of(n * C, C)
        qc = q_ref[pl.ds(start, C), :]
        kc = k_ref[pl.ds(start, C), :]
        vc = v_ref[pl.ds(start, C), :]
        s = lax.dot_general(qc, kc, (((1,), (1,)), ((), ())),
                            preferred_element_type=F32) * dmask
        o = jnp.dot(s.astype(BF16), vc, preferred_element_type=F32)
        R = r_ref[...]
        qd = (qc.astype(F32) * from_start).astype(BF16)
        o = o + jnp.dot(qd, R.astype(BF16), preferred_element_type=F32)
        kd = (kc.astype(F32) * to_end).astype(BF16)
        kv = lax.dot_general(kd, vc, (((0,), (0,)), ((), ())),
                             preferred_element_type=F32)
        r_ref[...] = cd * R + kv
        ms = jnp.mean(o * o, axis=-1, keepdims=True)
        y = o * lax.rsqrt(ms + 1e-6) * gn_ref[...]
        y = y * g_ref[pl.ds(start, C), :].astype(F32)
        o_ref[pl.ds(start, C), :] = y.astype(o_ref.dtype)
        return carry

    lax.fori_loop(0, n_chunks, body, 0)


def _retention(u, dmask, to_end, from_start, cdec, gn, B, S):
    kern = functools.partial(_ret_kernel, n_chunks=S // RET_CHUNK)
    col = lambda off: (lambda b, h: (b, off + h))
    hspec = lambda shp: pl.BlockSpec((None,) + shp, lambda b, h: (h, 0, 0))
    return pl.pallas_call(
        kern,
        grid=(B, H_R),
        in_specs=[
            pl.BlockSpec((S, DK_R), col(16)),
            pl.BlockSpec((S, DK_R), col(20)),
            pl.BlockSpec((S, DV_R), col(24)),
            pl.BlockSpec((S, DV_R), col(28)),
            hspec((RET_CHUNK, RET_CHUNK)),
            hspec((RET_CHUNK, 1)),
            hspec((RET_CHUNK, 1)),
            hspec((1, 1)),
            _resident((1, DV_R)),
        ],
        out_specs=pl.BlockSpec((S, DV_R), lambda b, h: (b, h)),
        out_shape=jax.ShapeDtypeStruct((B * S, BRANCH_W), BF16),
        scratch_shapes=[pltpu.VMEM((DK_R, DV_R), F32)],
        compiler_params=_cparams(("arbitrary", "arbitrary")),
        name="retention",
    )(u, u, u, u, dmask, to_end, from_start, cdec, gn)


def _mix_ffn_kernel(x_ref, ya_ref, yb_ref, yc_ref, g0_ref, g1_ref, g2_ref,
                    wb_ref, wo_ref, n2_ref, wfi_ref, wfo_ref, o_ref, act_ref):
    y = g0_ref[...].astype(F32) * jnp.dot(ya_ref[...], wb_ref[0],
                                          preferred_element_type=F32)
    y = y + g1_ref[...].astype(F32) * jnp.dot(yb_ref[...], wb_ref[1],
                                              preferred_element_type=F32)
    y = y + g2_ref[...].astype(F32) * jnp.dot(yc_ref[...], wb_ref[2],
                                              preferred_element_type=F32)
    x1 = x_ref[...] + jnp.dot(y.astype(BF16), wo_ref[...],
                              preferred_element_type=F32)
    ms = jnp.mean(x1 * x1, axis=-1, keepdims=True)
    h2 = (x1 * lax.rsqrt(ms + 1e-6) * n2_ref[...]).astype(BF16)
    for c in range(D_FF // FF_CHUNK + (1 if D_FF % FF_CHUNK else 0)):
        lo = c * FF_CHUNK
        hi = min(D_FF, lo + FF_CHUNK)
        g = jnp.dot(h2, wfi_ref[:, lo:hi], preferred_element_type=F32)
        up = jnp.dot(h2, wfi_ref[:, D_FF + lo:D_FF + hi],
                     preferred_element_type=F32)
        act_ref[:, lo:hi] = (g * _sigmoid(g) * up).astype(BF16)
    o_ref[...] = x1 + jnp.dot(act_ref[...], wfo_ref[...],
                              preferred_element_type=F32)


def _mix_ffn(x2, ya, yb, yc, u, wb, wo, n2, wfi, wfo):
    T = x2.shape[0]
    tm = TM_PROJ
    gate_blk = U_GATE_OFF // D_MODEL
    tok = lambda w: pl.BlockSpec((tm, w), lambda i: (i, 0))
    gate = lambda n: pl.BlockSpec((tm, D_MODEL), lambda i: (i, gate_blk + n))
    return pl.pallas_call(
        _mix_ffn_kernel,
        grid=(T // tm,),
        in_specs=[
            tok(D_MODEL), tok(BRANCH_W), tok(BRANCH_W), tok(BRANCH_W),
            gate(0), gate(1), gate(2),
            _resident((N_BRANCH, BRANCH_W, D_MODEL)),
            _resident((D_MODEL, D_MODEL)),
            _resident((1, D_MODEL)),
            _resident((D_MODEL, 2 * D_FF)),
            _resident((D_FF, D_MODEL)),
        ],
        out_specs=tok(D_MODEL),
        out_shape=jax.ShapeDtypeStruct((T, D_MODEL), F32),
        scratch_shapes=[pltpu.VMEM((tm, D_FF), BF16)],
        compiler_params=_cparams(("arbitrary",)),
        name="mix_ffn",
    )(x2, ya, yb, yc, u, u, u, wb, wo, n2, wfi, wfo)


def _retention_tables():
    C = RET_CHUNK
    log_g = jnp.log(1.0 - 2.0 ** (-5.0 - jnp.arange(H_R, dtype=F32)))
    idx = jnp.arange(C)
    diff = idx[:, None] - idx[None, :]
    scale = DK_R ** -0.5
    dmask = jnp.where(diff >= 0,
                      jnp.exp(jnp.maximum(diff, 0).astype(F32)[None]
                              * log_g[:, None, None]), 0.0) * scale
    to_end = jnp.exp((C - 1 - idx).astype(F32)[None, :, None]
                     * log_g[:, None, None]) * scale
    from_start = jnp.exp((idx + 1).astype(F32)[None, :, None]
                         * log_g[:, None, None])
    cdec = jnp.exp(C * log_g)[:, None, None]
    return dmask, to_end, from_start, cdec


def kernel(x, norm1_g, w_in, q_norm_g, k_norm_g, diff_lambda, attn_sub_g,
           conv_w, conv_b, conv_ln_g, conv_ln_b, ret_gn_g, b_gate, w_branch,
           w_out, norm2_g, w_ffn_in, w_ffn_out):
    B, S, D = x.shape
    depth = w_in.shape[0]
    assert D == D_MODEL and S % BQ == 0 and S % TC_CONV == 0
    assert (B * S) % TM_PROJ == 0

    grp = jnp.arange(SEG) // DK_A
    gsum = (grp[:, None] == grp[None, :]).astype(BF16)
    dmask, to_end, from_start, cdec = _retention_tables()
    reps = SEG // DK_A

    x2 = x.reshape(B * S, D)
    for l in range(depth):
        lam_init = 0.8 - 0.6 * math.exp(-0.3 * l)
        qg = (jnp.tile(q_norm_g[l].astype(F32), reps) * (DK_A ** -0.5))[None, :]
        kg = jnp.tile(k_norm_g[l].astype(F32), reps)[None, :]
        u = _inproj(x2, norm1_g[l][None, :], w_in[l].astype(BF16), qg, kg,
                    gsum, b_gate[l][None, :])
        ya = _attention(u, diff_lambda[l], attn_sub_g[l][None, :], lam_init,
                        B, S)
        yb = _conv(u, conv_w[l], conv_b[l][None, :], conv_ln_g[l][None, :],
                   conv_ln_b[l][None, :], B, S)
        yc = _retention(u, dmask, to_end, from_start, cdec,
                        ret_gn_g[l][None, :], B, S)
        x2 = _mix_ffn(x2, ya, yb, yc, u, w_branch[l].astype(BF16),
                      w_out[l].astype(BF16), norm2_g[l][None, :],
                      w_ffn_in[l].astype(BF16), w_ffn_out[l].astype(BF16))
    return x2.reshape(B, S, D)
```

```python
import functools
import math

import jax
import jax.numpy as jnp
from jax import lax
from jax.experimental import pallas as pl
from jax.experimental.pallas import tpu as pltpu

F32 = jnp.float32
BF16 = jnp.bfloat16

D_MODEL = 1024
BRANCH_W = D_MODEL // 2
H_A = 4
DV_A = BRANCH_W // H_A
DK_A = DV_A // 2
C_CONV = BRANCH_W
CONV_WIDTH = 31
H_R = 4
DK_R = BRANCH_W // H_R
DV_R = BRANCH_W // H_R
RET_CHUNK = 128
N_BRANCH = 3
D_FF = -(-8 * D_MODEL // (3 * 256)) * 256
D_IN = 4 * BRANCH_W + 2 * C_CONV + 3 * BRANCH_W + N_BRANCH * D_MODEL

SEG = 512
U_COLS = D_IN - C_CONV
U_GATE_OFF = 8 * SEG
VMEM_LIMIT = 56 * 1024 * 1024

TM_PROJ = 512
BQ = 512
TC_CONV = 512
CONV_HALO = 32
CONV_ROWS = 64
FF_CHUNK = 512
ATTN_BOUND_MAX = 30.0


def _cparams(sem):
    return pltpu.CompilerParams(dimension_semantics=sem,
                                vmem_limit_bytes=VMEM_LIMIT)


def _resident(shape):
    nd = len(shape)
    return pl.BlockSpec(shape, lambda *_: (0,) * nd,
                        pipeline_mode=pl.Buffered(1))


def _sigmoid(x):
    return 1.0 / (1.0 + jnp.exp(-x))


def _inproj_kernel(x_ref, g1_ref, w_ref, qg_ref, kg_ref, gsum_ref, bg_ref,
                   u_ref):
    xf = x_ref[...]
    ms = jnp.mean(xf * xf, axis=-1, keepdims=True)
    h = (xf * lax.rsqrt(ms + 1e-6) * g1_ref[...]).astype(BF16)

    def mm(c):
        return jnp.dot(h, w_ref[:, c * SEG:(c + 1) * SEG],
                       preferred_element_type=F32)

    def qk_norm(a, g):
        sq = a * a
        hi = sq.astype(BF16)
        lo = (sq - hi.astype(F32)).astype(BF16)
        ss = (jnp.dot(hi, gsum_ref[...], preferred_element_type=F32)
              + jnp.dot(lo, gsum_ref[...], preferred_element_type=F32))
        return a * lax.rsqrt(ss * (1.0 / DK_A) + 1e-6) * g

    u_ref[:, 0 * SEG:1 * SEG] = qk_norm(mm(0), qg_ref[...]).astype(BF16)
    u_ref[:, 1 * SEG:2 * SEG] = qk_norm(mm(1), kg_ref[...]).astype(BF16)
    u_ref[:, 2 * SEG:3 * SEG] = mm(2).astype(BF16)
    u_ref[:, 3 * SEG:4 * SEG] = (mm(3) * _sigmoid(mm(4))).astype(BF16)
    for c in (5, 6, 7):
        u_ref[:, (c - 1) * SEG:c * SEG] = mm(c).astype(BF16)
    rg = mm(8)
    u_ref[:, 7 * SEG:8 * SEG] = (rg * _sigmoid(rg)).astype(BF16)
    for c in range(9, 15):
        b = bg_ref[:, (c - 9) * SEG:(c - 8) * SEG]
        u_ref[:, (c - 1) * SEG:c * SEG] = _sigmoid(mm(c) + b).astype(BF16)


def _inproj(x2, g1, w, qg, kg, gsum, bg):
    T = x2.shape[0]
    return pl.pallas_call(
        _inproj_kernel,
        grid=(T // TM_PROJ,),
        in_specs=[
            pl.BlockSpec((TM_PROJ, D_MODEL), lambda i: (i, 0)),
            _resident((1, D_MODEL)),
            _resident((D_MODEL, D_IN)),
            _resident((1, SEG)),
            _resident((1, SEG)),
            _resident((SEG, SEG)),
            _resident((1, N_BRANCH * D_MODEL)),
        ],
        out_specs=pl.BlockSpec((TM_PROJ, U_COLS), lambda i: (i, 0)),
        out_shape=jax.ShapeDtypeStruct((T, U_COLS), BF16),
        compiler_params=_cparams(("arbitrary",)),
        name="inproj",
    )(x2, g1, w, qg, kg, gsum, bg)


def _attn_kernel(q_ref, k_ref, v_ref, dl_ref, sg_ref, o_ref,
                 m_ref, l_ref, acc_ref, *, lam_init):
    h = pl.program_id(1)
    i = pl.program_id(2)
    hf = (h + 1).astype(F32)
    slope = jnp.exp2(jnp.full((1, 1), -2.0, F32) * hf)

    q = q_ref[...]
    lane = lax.broadcasted_iota(jnp.int32, q.shape, 1)
    zero = jnp.zeros_like(q)
    qs = (jnp.where(lane < DK_A, q, zero), jnp.where(lane >= DK_A, q, zero))

    row = lax.broadcasted_iota(jnp.int32, (BQ, BQ), 0)
    col = lax.broadcasted_iota(jnp.int32, (BQ, BQ), 1)
    rel = (col - row).astype(F32) * slope
    rel_diag = jnp.where(col <= row, rel, -1e30)

    m_ref[...] = jnp.full(m_ref.shape, -1e30, F32)
    l_ref[...] = jnp.zeros(l_ref.shape, F32)
    acc_ref[...] = jnp.zeros(acc_ref.shape, F32)

    def step(j, bias):
        start = pl.multiple_of(j * BQ, BQ)
        kb = k_ref[pl.ds(start, BQ), :]
        vb = v_ref[pl.ds(start, BQ), :]
        c = slope * ((i - j) * BQ).astype(F32)
        for mi in range(2):
            s = lax.dot_general(qs[mi], kb, (((1,), (1,)), ((), ())),
                                preferred_element_type=F32) + bias
            m_old = m_ref[mi]
            m_new = jnp.maximum(m_old, jnp.max(s, axis=-1, keepdims=True) - c)
            p = jnp.exp(s - (m_new + c))
            alpha = jnp.exp(m_old - m_new)
            l_ref[mi] = alpha * l_ref[mi] + jnp.sum(p, axis=-1, keepdims=True)
            acc_ref[mi] = alpha * acc_ref[mi] + jnp.dot(
                p.astype(BF16), vb, preferred_element_type=F32)
            m_ref[mi] = m_new

    def body(j, carry):
        step(j, rel)
        return carry

    lax.fori_loop(0, i, body, 0)
    step(i, rel_diag)

    dl = dl_ref[...]
    lam = (jnp.exp(jnp.sum(dl[0:1] * dl[1:2], axis=-1, keepdims=True))
           - jnp.exp(jnp.sum(dl[2:3] * dl[3:4], axis=-1, keepdims=True))
           + lam_init)
    o = acc_ref[0] / l_ref[0] - lam * (acc_ref[1] / l_ref[1])
    ms = jnp.mean(o * o, axis=-1, keepdims=True)
    y = o * lax.rsqrt(ms + 1e-6) * sg_ref[...] * (1.0 - lam_init)
    o_ref[...] = y.astype(o_ref.dtype)


def _attn_bounded_kernel(q_ref, qa_ref, k_ref, ka_ref, v_ref, dl_ref, sg_ref,
                         o_ref, acc_ref, *, lam_init):
    i = pl.program_id(2)
    q = q_ref[...]
    qa = qa_ref[...]
    lane = lax.broadcasted_iota(jnp.int32, q.shape, 1)
    zero = jnp.zeros_like(q)
    q_both = jnp.concatenate([
        jnp.concatenate([jnp.where(lane < DK_A, q, zero), qa], axis=1),
        jnp.concatenate([jnp.where(lane >= DK_A, q, zero), qa], axis=1),
    ], axis=0)
    ones_col = (lane == 0).astype(BF16)

    acc_ref[...] = jnp.zeros(acc_ref.shape, F32)

    def step(j, diagonal):
        start = pl.multiple_of(j * BQ, BQ)
        kb = jnp.concatenate([k_ref[pl.ds(start, BQ), :],
                              ka_ref[pl.ds(start, BQ), :]], axis=1)
        vb = jnp.concatenate([v_ref[pl.ds(start, BQ), :], ones_col], axis=1)
        s = lax.dot_general(q_both, kb, (((1,), (1,)), ((), ())),
                            preferred_element_type=F32)
        p = jnp.exp(s)
        if diagonal:
            row = lax.broadcasted_iota(jnp.int32, s.shape, 0) & (BQ - 1)
            col = lax.broadcasted_iota(jnp.int32, s.shape, 1)
            p = jnp.where(col <= row, p, 0.0)
        acc_ref[...] += jnp.dot(p.astype(BF16), vb, preferred_element_type=F32)

    def body(j, carry):
        step(j, False)
        return carry

    lax.fori_loop(0, i, body, 0)
    step(i, True)

    dl = dl_ref[...]
    lam = (jnp.exp(jnp.sum(dl[0:1] * dl[1:2], axis=-1, keepdims=True))
           - jnp.exp(jnp.sum(dl[2:3] * dl[3:4], axis=-1, keepdims=True))
           + lam_init)
    o1 = acc_ref[0:BQ, 0:DV_A] / acc_ref[0:BQ, DV_A:DV_A + 1]
    o2 = acc_ref[BQ:2 * BQ, 0:DV_A] / acc_ref[BQ:2 * BQ, DV_A:DV_A + 1]
    o = o1 - lam * o2
    ms = jnp.mean(o * o, axis=-1, keepdims=True)
    y = o * lax.rsqrt(ms + 1e-6) * sg_ref[...] * (1.0 - lam_init)
    o_ref[...] = y.astype(o_ref.dtype)


def _attention_bounded(u, qa, ka, dl, sg, lam_init, B, S):
    nq = S // BQ
    kern = functools.partial(_attn_bounded_kernel, lam_init=lam_init)
    return pl.pallas_call(
        kern,
        grid=(B, H_A, nq),
        in_specs=[
            pl.BlockSpec((BQ, DV_A), lambda b, h, i: (b * nq + i, h)),
            pl.BlockSpec((None, BQ, DV_A), lambda b, h, i: (h, i, 0)),
            pl.BlockSpec((S, DV_A), lambda b, h, i: (b, H_A + h)),
            pl.BlockSpec((None, S, DV_A), lambda b, h, i: (h, 0, 0)),
            pl.BlockSpec((S, DV_A), lambda b, h, i: (b, 2 * H_A + h)),
            _resident((4, DK_A)),
            _resident((1, DV_A)),
        ],
        out_specs=pl.BlockSpec((BQ, DV_A), lambda b, h, i: (b * nq + i, h)),
        out_shape=jax.ShapeDtypeStruct((B * S, BRANCH_W), BF16),
        scratch_shapes=[pltpu.VMEM((2 * BQ, 2 * DV_A), F32)],
        compiler_params=_cparams(("arbitrary",) * 3),
        name="diff_attn_bounded",
    )(u, qa, u, ka, u, dl, sg)


def _alibi_tables(S, bound):
    pos = jnp.arange(S, dtype=jnp.int32)
    lo = (pos % 256).astype(F32)[None, :]
    hi = (pos - pos % 256).astype(F32)[None, :]
    slopes = (2.0 ** (-8.0 * (jnp.arange(H_A, dtype=F32) + 1.0) / H_A))[:, None]
    one = jnp.ones((H_A, S), F32)
    qcols = [-slopes * hi, -slopes * lo, one, one, -bound * one]
    kcols = [one, one, slopes * hi, slopes * lo, one]
    pad = jnp.zeros((H_A, S, DV_A - len(qcols)), F32)
    qa = jnp.concatenate([jnp.stack(qcols, axis=-1), pad], axis=-1).astype(BF16)
    ka = jnp.concatenate([jnp.stack(kcols, axis=-1), pad], axis=-1).astype(BF16)
    return qa, ka


def _attention(u, dl, sg, lam_init, B, S):
    nq = S // BQ
    kern = functools.partial(_attn_kernel, lam_init=lam_init)
    return pl.pallas_call(
        kern,
        grid=(B, H_A, nq),
        in_specs=[
            pl.BlockSpec((BQ, DV_A), lambda b, h, i: (b * nq + i, h)),
            pl.BlockSpec((S, DV_A), lambda b, h, i: (b, H_A + h)),
            pl.BlockSpec((S, DV_A), lambda b, h, i: (b, 2 * H_A + h)),
            _resident((4, DK_A)),
            _resident((1, DV_A)),
        ],
        out_specs=pl.BlockSpec((BQ, DV_A), lambda b, h, i: (b * nq + i, h)),
        out_shape=jax.ShapeDtypeStruct((B * S, BRANCH_W), BF16),
        scratch_shapes=[
            pltpu.VMEM((2, BQ, 1), F32),
            pltpu.VMEM((2, BQ, 1), F32),
            pltpu.VMEM((2, BQ, DV_A), F32),
        ],
        compiler_params=_cparams(("arbitrary",) * 3),
        name="diff_attn",
    )(u, u, u, dl, sg)


def _conv_kernel(z_ref, w_ref, b_ref, lg_ref, lb_ref, o_ref, zbuf):
    s = pl.program_id(1)

    @pl.when(s == 0)
    def _():
        zbuf[0:CONV_HALO, :] = jnp.zeros((CONV_HALO, C_CONV), F32)

    zbuf[CONV_HALO:CONV_HALO + TC_CONV, :] = z_ref[...].astype(F32)

    off = CONV_HALO - (CONV_WIDTH - 1)
    for r in range(TC_CONV // CONV_ROWS):
        base = r * CONV_ROWS
        acc = jnp.broadcast_to(b_ref[...], (CONV_ROWS, C_CONV))
        for w in range(CONV_WIDTH):
            acc = acc + zbuf[base + off + w:base + off + w + CONV_ROWS, :] * w_ref[w:w + 1, :]
        mu = jnp.mean(acc, axis=-1, keepdims=True)
        xc = acc - mu
        var = jnp.mean(xc * xc, axis=-1, keepdims=True)
        y = xc * lax.rsqrt(var + 1e-5) * lg_ref[...] + lb_ref[...]
        o_ref[base:base + CONV_ROWS, :] = (y * _sigmoid(y)).astype(o_ref.dtype)

    zbuf[0:CONV_HALO, :] = zbuf[TC_CONV:TC_CONV + CONV_HALO, :]


def _conv(u, w, b, lg, lb, B, S):
    ns = S // TC_CONV
    return pl.pallas_call(
        _conv_kernel,
        grid=(B, ns),
        in_specs=[
            pl.BlockSpec((TC_CONV, C_CONV), lambda b, s: (b * ns + s, 3)),
            _resident((CONV_WIDTH, C_CONV)),
            _resident((1, C_CONV)),
            _resident((1, C_CONV)),
            _resident((1, C_CONV)),
        ],
        out_specs=pl.BlockSpec((TC_CONV, C_CONV), lambda b, s: (b * ns + s, 0)),
        out_shape=jax.ShapeDtypeStruct((B * S, BRANCH_W), BF16),
        scratch_shapes=[pltpu.VMEM((CONV_HALO + TC_CONV, C_CONV), F32)],
        compiler_params=_cparams(("arbitrary", "arbitrary")),
        name="conformer_conv",
    )(u, w, b, lg, lb)


def _ret_kernel(q_ref, k_ref, v_ref, g_ref, dm_ref, te_ref, fs_ref, cd_ref,
                gn_ref, o_ref, r_ref, *, n_chunks):
    C = RET_CHUNK
    r_ref[...] = jnp.zeros(r_ref.shape, F32)
    dmask = dm_ref[...]
    to_end = te_ref[...]
    from_start = fs_ref[...]
    cd = cd_ref[...]

    def body(n, carry):
        start = pl.multiple_of(n * C, C)
        qc = q_ref[pl.ds(start, C), :]
        kc = k_ref[pl.ds(start, C), :]
        vc = v_ref[pl.ds(start, C), :]
        s = lax.dot_general(qc, kc, (((1,), (1,)), ((), ())),
                            preferred_element_type=F32) * dmask
        o = jnp.dot(s.astype(BF16), vc, preferred_element_type=F32)
        R = r_ref[...]
        qd = (qc.astype(F32) * from_start).astype(BF16)
        o = o + jnp.dot(qd, R.astype(BF16), preferred_element_type=F32)
        kd = (kc.astype(F32) * to_end).astype(BF16)
        kv = lax.dot_general(kd, vc, (((0,), (0,)), ((), ())),
                             preferred_element_type=F32)
        r_ref[...] = cd * R + kv
        ms = jnp.mean(o * o, axis=-1, keepdims=True)
        y = o * lax.rsqrt(ms + 1e-6) * gn_ref[...]
        y = y * g_ref[pl.ds(start, C), :].astype(F32)
        o_ref[pl.ds(start, C), :] = y.astype(o_ref.dtype)
        return carry

    lax.fori_loop(0, n_chunks, body, 0)


def _retention(u, dmask, to_end, from_start, cdec, gn, B, S):
    kern = functools.partial(_ret_kernel, n_chunks=S // RET_CHUNK)
    col = lambda off: (lambda b, h: (b, off + h))
    hspec = lambda shp: pl.BlockSpec((None,) + shp, lambda b, h: (h, 0, 0))
    return pl.pallas_call(
        kern,
        grid=(B, H_R),
        in_specs=[
            pl.BlockSpec((S, DK_R), col(16)),
            pl.BlockSpec((S, DK_R), col(20)),
            pl.BlockSpec((S, DV_R), col(24)),
            pl.BlockSpec((S, DV_R), col(28)),
            hspec((RET_CHUNK, RET_CHUNK)),
            hspec((RET_CHUNK, 1)),
            hspec((RET_CHUNK, 1)),
            hspec((1, 1)),
            _resident((1, DV_R)),
        ],
        out_specs=pl.BlockSpec((S, DV_R), lambda b, h: (b, h)),
        out_shape=jax.ShapeDtypeStruct((B * S, BRANCH_W), BF16),
        scratch_shapes=[pltpu.VMEM((DK_R, DV_R), F32)],
        compiler_params=_cparams(("arbitrary", "arbitrary")),
        name="retention",
    )(u, u, u, u, dmask, to_end, from_start, cdec, gn)


def _mix_ffn_kernel(x_ref, ya_ref, yb_ref, yc_ref, g0_ref, g1_ref, g2_ref,
                    wb_ref, wo_ref, n2_ref, wfi_ref, wfo_ref, o_ref, act_ref):
    y = g0_ref[...].astype(F32) * jnp.dot(ya_ref[...], wb_ref[0],
                                          preferred_element_type=F32)
    y = y + g1_ref[...].astype(F32) * jnp.dot(yb_ref[...], wb_ref[1],
                                              preferred_element_type=F32)
    y = y + g2_ref[...].astype(F32) * jnp.dot(yc_ref[...], wb_ref[2],
                                              preferred_element_type=F32)
    x1 = x_ref[...] + jnp.dot(y.astype(BF16), wo_ref[...],
                              preferred_element_type=F32)
    ms = jnp.mean(x1 * x1, axis=-1, keepdims=True)
    h2 = (x1 * lax.rsqrt(ms + 1e-6) * n2_ref[...]).astype(BF16)
    for c in range(D_FF // FF_CHUNK + (1 if D_FF % FF_CHUNK else 0)):
        lo = c * FF_CHUNK
        hi = min(D_FF, lo + FF_CHUNK)
        g = jnp.dot(h2, wfi_ref[:, lo:hi], preferred_element_type=F32)
        up = jnp.dot(h2, wfi_ref[:, D_FF + lo:D_FF + hi],
                     preferred_element_type=F32)
        act_ref[:, lo:hi] = (g * _sigmoid(g) * up).astype(BF16)
    o_ref[...] = x1 + jnp.dot(act_ref[...], wfo_ref[...],
                              preferred_element_type=F32)


def _mix_ffn(x2, ya, yb, yc, u, wb, wo, n2, wfi, wfo):
    T = x2.shape[0]
    tm = TM_PROJ
    gate_blk = U_GATE_OFF // D_MODEL
    tok = lambda w: pl.BlockSpec((tm, w), lambda i: (i, 0))
    gate = lambda n: pl.BlockSpec((tm, D_MODEL), lambda i: (i, gate_blk + n))
    return pl.pallas_call(
        _mix_ffn_kernel,
        grid=(T // tm,),
        in_specs=[
            tok(D_MODEL), tok(BRANCH_W), tok(BRANCH_W), tok(BRANCH_W),
            gate(0), gate(1), gate(2),
            _resident((N_BRANCH, BRANCH_W, D_MODEL)),
            _resident((D_MODEL, D_MODEL)),
            _resident((1, D_MODEL)),
            _resident((D_MODEL, 2 * D_FF)),
            _resident((D_FF, D_MODEL)),
        ],
        out_specs=tok(D_MODEL),
        out_shape=jax.ShapeDtypeStruct((T, D_MODEL), F32),
        scratch_shapes=[pltpu.VMEM((tm, D_FF), BF16)],
        compiler_params=_cparams(("arbitrary",)),
        name="mix_ffn",
    )(x2, ya, yb, yc, u, u, u, wb, wo, n2, wfi, wfo)


def _retention_tables():
    C = RET_CHUNK
    log_g = jnp.log(1.0 - 2.0 ** (-5.0 - jnp.arange(H_R, dtype=F32)))
    idx = jnp.arange(C)
    diff = idx[:, None] - idx[None, :]
    scale = DK_R ** -0.5
    dmask = jnp.where(diff >= 0,
                      jnp.exp(jnp.maximum(diff, 0).astype(F32)[None]
                              * log_g[:, None, None]), 0.0) * scale
    to_end = jnp.exp((C - 1 - idx).astype(F32)[None, :, None]
                     * log_g[:, None, None]) * scale
    from_start = jnp.exp((idx + 1).astype(F32)[None, :, None]
                         * log_g[:, None, None])
    cdec = jnp.exp(C * log_g)[:, None, None]
    return dmask, to_end, from_start, cdec


def kernel(x, norm1_g, w_in, q_norm_g, k_norm_g, diff_lambda, attn_sub_g,
           conv_w, conv_b, conv_ln_g, conv_ln_b, ret_gn_g, b_gate, w_branch,
           w_out, norm2_g, w_ffn_in, w_ffn_out):
    B, S, D = x.shape
    depth = w_in.shape[0]
    assert D == D_MODEL and S % BQ == 0 and S % TC_CONV == 0
    assert (B * S) % TM_PROJ == 0

    grp = jnp.arange(SEG) // DK_A
    gsum = (grp[:, None] == grp[None, :]).astype(BF16)
    dmask, to_end, from_start, cdec = _retention_tables()
    reps = SEG // DK_A

    x2 = x.reshape(B * S, D)
    for l in range(depth):
        lam_init = 0.8 - 0.6 * math.exp(-0.3 * l)
        qg = (jnp.tile(q_norm_g[l].astype(F32), reps) * (DK_A ** -0.5))[None, :]
        kg = jnp.tile(k_norm_g[l].astype(F32), reps)[None, :]
        u = _inproj(x2, norm1_g[l][None, :], w_in[l].astype(BF16), qg, kg,
                    gsum, b_gate[l][None, :])
        bound = (math.sqrt(DK_A) * jnp.max(jnp.abs(q_norm_g[l].astype(F32)))
                 * jnp.max(jnp.abs(k_norm_g[l].astype(F32))))
        dl = diff_lambda[l].astype(F32)
        sg = attn_sub_g[l].astype(F32)[None, :]

        def attn_bounded(u=u, bound=bound, dl=dl, sg=sg, lam_init=lam_init):
            qa, ka = _alibi_tables(S, bound)
            return _attention_bounded(u, qa, ka, dl, sg, lam_init, B, S)

        def attn_online(u=u, dl=dl, sg=sg, lam_init=lam_init):
            return _attention(u, dl, sg, lam_init, B, S)

        ya = lax.cond(bound <= ATTN_BOUND_MAX, attn_bounded, attn_online)
        yb = _conv(u, conv_w[l], conv_b[l][None, :], conv_ln_g[l][None, :],
                   conv_ln_b[l][None, :], B, S)
        yc = _retention(u, dmask, to_end, from_start, cdec,
                        ret_gn_g[l][None, :], B, S)
        x2 = _mix_ffn(x2, ya, yb, yc, u, w_branch[l].astype(BF16),
                      w_out[l].astype(BF16), norm2_g[l][None, :],
                      w_ffn_in[l].astype(BF16), w_ffn_out[l].astype(BF16))
    return x2.reshape(B, S, D)
```

```python
import functools
import math

import jax
import jax.numpy as jnp
from jax import lax
from jax.experimental import pallas as pl
from jax.experimental.pallas import tpu as pltpu

F32 = jnp.float32
BF16 = jnp.bfloat16

D_MODEL = 1024
BRANCH_W = D_MODEL // 2
H_A = 4
DV_A = BRANCH_W // H_A
DK_A = DV_A // 2
C_CONV = BRANCH_W
CONV_WIDTH = 31
H_R = 4
DK_R = BRANCH_W // H_R
DV_R = BRANCH_W // H_R
RET_KCHUNK = 256
N_BRANCH = 3
D_FF = -(-8 * D_MODEL // (3 * 256)) * 256
D_IN = 4 * BRANCH_W + 2 * C_CONV + 3 * BRANCH_W + N_BRANCH * D_MODEL

SEG = 512
U_COLS = D_IN - C_CONV
U_GATE_OFF = 8 * SEG
VMEM_LIMIT = 56 * 1024 * 1024

TM_PROJ = 512
BQ = 512
BQ_B = 1024
BD_B = BQ_B // 2
POS_SPLIT = 256
TC_CONV = 512
CONV_HALO = 32
CONV_ROWS = 64
FF_CHUNK = 512
ATTN_BOUND_MAX = 30.0


def _cparams(sem):
    return pltpu.CompilerParams(dimension_semantics=sem,
                                vmem_limit_bytes=VMEM_LIMIT)


def _resident(shape):
    nd = len(shape)
    return pl.BlockSpec(shape, lambda *_: (0,) * nd,
                        pipeline_mode=pl.Buffered(1))


def _layer_resident(shape, l):
    nd = len(shape)
    return pl.BlockSpec((None,) + tuple(shape), lambda *_: (l,) + (0,) * nd,
                        pipeline_mode=pl.Buffered(1))


def _sigmoid(x):
    return 1.0 / (1.0 + jnp.exp(-x))


def _inproj_kernel(x_ref, g1_ref, w_ref, qg_ref, kg_ref, gsum_ref, bg_ref,
                   u_ref):
    xf = x_ref[...]
    ms = jnp.mean(xf * xf, axis=-1, keepdims=True)
    h = (xf * lax.rsqrt(ms + 1e-6) * g1_ref[...]).astype(BF16)

    def mm(c):
        return jnp.dot(h, w_ref[:, c * SEG:(c + 1) * SEG],
                       preferred_element_type=F32)

    def qk_norm(a, g):
        sq = a * a
        hi = sq.astype(BF16)
        lo = (sq - hi.astype(F32)).astype(BF16)
        ss = (jnp.dot(hi, gsum_ref[...], preferred_element_type=F32)
              + jnp.dot(lo, gsum_ref[...], preferred_element_type=F32))
        return a * lax.rsqrt(ss * (1.0 / DK_A) + 1e-6) * g

    u_ref[:, 0 * SEG:1 * SEG] = qk_norm(mm(0), qg_ref[...]).astype(BF16)
    u_ref[:, 1 * SEG:2 * SEG] = qk_norm(mm(1), kg_ref[...]).astype(BF16)
    u_ref[:, 2 * SEG:3 * SEG] = mm(2).astype(BF16)
    u_ref[:, 3 * SEG:4 * SEG] = (mm(3) * _sigmoid(mm(4))).astype(BF16)
    for c in (5, 6, 7):
        u_ref[:, (c - 1) * SEG:c * SEG] = mm(c).astype(BF16)
    rg = mm(8)
    u_ref[:, 7 * SEG:8 * SEG] = (rg * _sigmoid(rg)).astype(BF16)
    for c in range(9, 15):
        b = bg_ref[:, (c - 9) * SEG:(c - 8) * SEG]
        u_ref[:, (c - 1) * SEG:c * SEG] = _sigmoid(mm(c) + b).astype(BF16)


def _inproj(x2, g1, w_all, l, qg, kg, gsum, bg):
    T = x2.shape[0]
    return pl.pallas_call(
        _inproj_kernel,
        grid=(T // TM_PROJ,),
        in_specs=[
            pl.BlockSpec((TM_PROJ, D_MODEL), lambda i: (i, 0)),
            _resident((1, D_MODEL)),
            _layer_resident((D_MODEL, D_IN), l),
            _resident((1, SEG)),
            _resident((1, SEG)),
            _resident((SEG, SEG)),
            _resident((1, N_BRANCH * D_MODEL)),
        ],
        out_specs=pl.BlockSpec((TM_PROJ, U_COLS), lambda i: (i, 0)),
        out_shape=jax.ShapeDtypeStruct((T, U_COLS), BF16),
        compiler_params=_cparams(("arbitrary",)),
        name="inproj",
    )(x2, g1, w_all, qg, kg, gsum, bg)


def _diff_attn_epilogue(o1, o2, dl_ref, sg_ref, lam_init):
    dl = dl_ref[...]
    lam = (jnp.exp(jnp.sum(dl[0:1] * dl[1:2], axis=-1, keepdims=True))
           - jnp.exp(jnp.sum(dl[2:3] * dl[3:4], axis=-1, keepdims=True))
           + lam_init)
    o = o1 - lam * o2
    ms = jnp.mean(o * o, axis=-1, keepdims=True)
    return o * lax.rsqrt(ms + 1e-6) * sg_ref[...] * (1.0 - lam_init)


def _alibi_slope(h):
    hf = (h + 1).astype(F32)
    return jnp.exp2(jnp.full((1, 1), -2.0, F32) * hf)


def _attn_kernel(q_ref, k_ref, v_ref, dl_ref, sg_ref, o_ref,
                 m_ref, l_ref, acc_ref, *, lam_init):
    h = pl.program_id(1)
    i = pl.program_id(2)
    slope = _alibi_slope(h)

    q = q_ref[...]
    lane = lax.broadcasted_iota(jnp.int32, q.shape, 1)
    zero = jnp.zeros_like(q)
    qs = (jnp.where(lane < DK_A, q, zero), jnp.where(lane >= DK_A, q, zero))

    row = lax.broadcasted_iota(jnp.int32, (BQ, BQ), 0)
    col = lax.broadcasted_iota(jnp.int32, (BQ, BQ), 1)
    rel = (col - row).astype(F32) * slope
    rel_diag = jnp.where(col <= row, rel, -1e30)

    m_ref[...] = jnp.full(m_ref.shape, -1e30, F32)
    l_ref[...] = jnp.zeros(l_ref.shape, F32)
    acc_ref[...] = jnp.zeros(acc_ref.shape, F32)

    def step(j, bias):
        start = pl.multiple_of(j * BQ, BQ)
        kb = k_ref[pl.ds(start, BQ), :]
        vb = v_ref[pl.ds(start, BQ), :]
        c = slope * ((i - j) * BQ).astype(F32)
        for mi in range(2):
            s = lax.dot_general(qs[mi], kb, (((1,), (1,)), ((), ())),
                                preferred_element_type=F32) + bias
            m_old = m_ref[mi]
            m_new = jnp.maximum(m_old, jnp.max(s, axis=-1, keepdims=True) - c)
            p = jnp.exp(s - (m_new + c))
            alpha = jnp.exp(m_old - m_new)
            l_ref[mi] = alpha * l_ref[mi] + jnp.sum(p, axis=-1, keepdims=True)
            acc_ref[mi] = alpha * acc_ref[mi] + jnp.dot(
                p.astype(BF16), vb, preferred_element_type=F32)
            m_ref[mi] = m_new

    def body(j, carry):
        step(j, rel)
        return carry

    lax.fori_loop(0, i, body, 0)
    step(i, rel_diag)

    y = _diff_attn_epilogue(acc_ref[0] / l_ref[0], acc_ref[1] / l_ref[1],
                            dl_ref, sg_ref, lam_init)
    o_ref[...] = y.astype(o_ref.dtype)


def _attention(u, dl, sg, lam_init, B, S):
    nq = S // BQ
    kern = functools.partial(_attn_kernel, lam_init=lam_init)
    return pl.pallas_call(
        kern,
        grid=(B, H_A, nq),
        in_specs=[
            pl.BlockSpec((BQ, DV_A), lambda b, h, i: (b * nq + i, h)),
            pl.BlockSpec((S, DV_A), lambda b, h, i: (b, H_A + h)),
            pl.BlockSpec((S, DV_A), lambda b, h, i: (b, 2 * H_A + h)),
            _resident((4, DK_A)),
            _resident((1, DV_A)),
        ],
        out_specs=pl.BlockSpec((BQ, DV_A), lambda b, h, i: (b * nq + i, h)),
        out_shape=jax.ShapeDtypeStruct((B * S, BRANCH_W), BF16),
        scratch_shapes=[
            pltpu.VMEM((2, BQ, 1), F32),
            pltpu.VMEM((2, BQ, 1), F32),
            pltpu.VMEM((2, BQ, DV_A), F32),
        ],
        compiler_params=_cparams(("arbitrary",) * 3),
        name="diff_attn",
    )(u, u, u, dl, sg)


def _attn_bounded_kernel(bound_ref, q_ref, k_ref, v_ref, dl_ref, sg_ref, o_ref,
                         ka_ref, acc_ref, *, lam_init):
    h = pl.program_id(1)
    i = pl.program_id(2)
    slope = _alibi_slope(h)
    S = k_ref.shape[0]

    @pl.when(i == 0)
    def _():
        lane = lax.broadcasted_iota(jnp.int32, (S, DV_A), 1)
        pos = lax.broadcasted_iota(jnp.int32, (S, DV_A), 0)
        lo = pos & (POS_SPLIT - 1)
        hi = pos - lo
        ka = jnp.where(lane == 2, slope * hi.astype(F32),
                       jnp.where(lane == 3, slope * lo.astype(F32),
                                 jnp.where((lane < 2) | (lane == 4), 1.0, 0.0)))
        ka_ref[...] = ka.astype(BF16)

    lane = lax.broadcasted_iota(jnp.int32, (BD_B, DV_A), 1)
    neg_bound = jnp.broadcast_to(-bound_ref[...], (BD_B, DV_A))
    blocks = []
    for half in range(2):
        q = q_ref[half * BD_B:(half + 1) * BD_B, :]
        pos = (lax.broadcasted_iota(jnp.int32, (BD_B, DV_A), 0)
               + (i * BQ_B + half * BD_B))
        lo = pos & (POS_SPLIT - 1)
        hi = pos - lo
        qa = jnp.where(lane == 0, -slope * hi.astype(F32),
                       jnp.where(lane == 1, -slope * lo.astype(F32),
                                 jnp.where((lane == 2) | (lane == 3), 1.0,
                                           jnp.where(lane == 4, neg_bound, 0.0))))
        qa = qa.astype(BF16)
        zero = jnp.zeros_like(q)
        blocks.append(jnp.concatenate([jnp.where(lane < DK_A, q, zero), qa], axis=1))
        blocks.append(jnp.concatenate([jnp.where(lane >= DK_A, q, zero), qa], axis=1))
    q_both = jnp.concatenate(blocks, axis=0)
    q_second = jnp.concatenate(blocks[2:], axis=0)

    acc_ref[...] = jnp.zeros(acc_ref.shape, F32)

    def scores(qrows, start, n):
        kb = jnp.concatenate([k_ref[pl.ds(start, n), :],
                              ka_ref[pl.ds(start, n), :]], axis=1)
        return lax.dot_general(qrows, kb, (((1,), (1,)), ((), ())),
                               preferred_element_type=F32)

    def weighted_values(p, start, n):
        lane_v = lax.broadcasted_iota(jnp.int32, (n, DV_A), 1)
        vb = jnp.concatenate([v_ref[pl.ds(start, n), :],
                              (lane_v == 0).astype(BF16)], axis=1)
        return jnp.dot(p.astype(BF16), vb, preferred_element_type=F32)

    def body(j, carry):
        start = pl.multiple_of(j * BQ_B, BQ_B)
        p = jnp.exp(scores(q_both, start, BQ_B))
        acc_ref[...] += weighted_values(p, start, BQ_B)
        return carry

    lax.fori_loop(0, i, body, 0)

    start_a = pl.multiple_of(i * BQ_B, BQ_B)
    s = scores(q_both, start_a, BD_B)
    row = lax.broadcasted_iota(jnp.int32, s.shape, 0)
    col = lax.broadcasted_iota(jnp.int32, s.shape, 1)
    keep = (row >= BQ_B) | (col <= (row & (BD_B - 1)))
    acc_ref[...] += weighted_values(jnp.where(keep, jnp.exp(s), 0.0),
                                    start_a, BD_B)

    start_b = pl.multiple_of(i * BQ_B + BD_B, BD_B)
    s = scores(q_second, start_b, BD_B)
    row = lax.broadcasted_iota(jnp.int32, s.shape, 0)
    col = lax.broadcasted_iota(jnp.int32, s.shape, 1)
    keep = col <= (row & (BD_B - 1))
    acc_ref[BQ_B:, :] += weighted_values(jnp.where(keep, jnp.exp(s), 0.0),
                                         start_b, BD_B)

    for half in range(2):
        r0 = half * BQ_B
        o1 = acc_ref[r0:r0 + BD_B, 0:DV_A] / acc_ref[r0:r0 + BD_B, DV_A:DV_A + 1]
        o2 = (acc_ref[r0 + BD_B:r0 + BQ_B, 0:DV_A]
              / acc_ref[r0 + BD_B:r0 + BQ_B, DV_A:DV_A + 1])
        y = _diff_attn_epilogue(o1, o2, dl_ref, sg_ref, lam_init)
        o_ref[half * BD_B:(half + 1) * BD_B, :] = y.astype(o_ref.dtype)


def _attention_bounded(u, bound, dl, sg, lam_init, B, S):
    nq = S // BQ_B
    kern = functools.partial(_attn_bounded_kernel, lam_init=lam_init)
    return pl.pallas_call(
        kern,
        grid=(B, H_A, nq),
        in_specs=[
            _resident((1, 1)),
            pl.BlockSpec((BQ_B, DV_A), lambda b, h, i: (b * nq + i, h)),
            pl.BlockSpec((S, DV_A), lambda b, h, i: (b, H_A + h)),
            pl.BlockSpec((S, DV_A), lambda b, h, i: (b, 2 * H_A + h)),
            _resident((4, DK_A)),
            _resident((1, DV_A)),
        ],
        out_specs=pl.BlockSpec((BQ_B, DV_A), lambda b, h, i: (b * nq + i, h)),
        out_shape=jax.ShapeDtypeStruct((B * S, BRANCH_W), BF16),
        scratch_shapes=[
            pltpu.VMEM((S, DV_A), BF16),
            pltpu.VMEM((2 * BQ_B, 2 * DV_A), F32),
        ],
        compiler_params=_cparams(("arbitrary",) * 3),
        name="diff_attn_bounded",
    )(bound, u, u, u, dl, sg)


def _conv_kernel(z_ref, w_ref, b_ref, lg_ref, lb_ref, o_ref, zbuf):
    s = pl.program_id(1)

    @pl.when(s == 0)
    def _():
        zbuf[0:CONV_HALO, :] = jnp.zeros((CONV_HALO, C_CONV), F32)

    zbuf[CONV_HALO:CONV_HALO + TC_CONV, :] = z_ref[...].astype(F32)

    off = CONV_HALO - (CONV_WIDTH - 1)
    n_slabs = (off + CONV_WIDTH - 1) // 8 + 1
    R = CONV_ROWS
    for r in range(TC_CONV // R):
        base = r * R
        acc = jnp.broadcast_to(b_ref[...], (R, C_CONV))
        for d in range(8):
            rows = R if d == 0 else R + 8
            y = None
            for a in range(n_slabs):
                k = 8 * a + d
                if off <= k < off + CONV_WIDTH:
                    term = (zbuf[base + 8 * a:base + 8 * a + rows, :]
                            * w_ref[k - off:k - off + 1, :])
                    y = term if y is None else y + term
            acc = acc + y[d:d + R, :]
        mu = jnp.mean(acc, axis=-1, keepdims=True)
        xc = acc - mu
        var = jnp.mean(xc * xc, axis=-1, keepdims=True)
        y = xc * lax.rsqrt(var + 1e-5) * lg_ref[...] + lb_ref[...]
        o_ref[base:base + CONV_ROWS, :] = (y * _sigmoid(y)).astype(o_ref.dtype)

    zbuf[0:CONV_HALO, :] = zbuf[TC_CONV:TC_CONV + CONV_HALO, :]


def _conv(u, w, b, lg, lb, B, S):
    ns = S // TC_CONV
    return pl.pallas_call(
        _conv_kernel,
        grid=(B, ns),
        in_specs=[
            pl.BlockSpec((TC_CONV, C_CONV), lambda b, s: (b * ns + s, 3)),
            _resident((CONV_WIDTH, C_CONV)),
            _resident((1, C_CONV)),
            _resident((1, C_CONV)),
            _resident((1, C_CONV)),
        ],
        out_specs=pl.BlockSpec((TC_CONV, C_CONV), lambda b, s: (b * ns + s, 0)),
        out_shape=jax.ShapeDtypeStruct((B * S, BRANCH_W), BF16),
        scratch_shapes=[pltpu.VMEM((CONV_HALO + TC_CONV, C_CONV), F32)],
        compiler_params=_cparams(("arbitrary", "arbitrary")),
        name="conformer_conv",
    )(u, w, b, lg, lb)


def _ret_kernel(q_ref, k_ref, v_ref, g_ref, dm_ref, te_ref, fs_ref, cd_ref,
                gn_ref, o_ref, *, n_chunks):
    C = RET_KCHUNK
    dmask = dm_ref[...]
    to_end = te_ref[...]
    from_start = fs_ref[...]
    cd = cd_ref[...]
    R = None
    for n in range(n_chunks):
        rows = slice(n * C, (n + 1) * C)
        qc = q_ref[rows, :]
        kc = k_ref[rows, :]
        vc = v_ref[rows, :]
        s = lax.dot_general(qc, kc, (((1,), (1,)), ((), ())),
                            preferred_element_type=F32) * dmask
        o = jnp.dot(s.astype(BF16), vc, preferred_element_type=F32)
        if R is not None:
            qd = (qc.astype(F32) * from_start).astype(BF16)
            o = o + jnp.dot(qd, R.astype(BF16), preferred_element_type=F32)
        if n + 1 < n_chunks:
            kd = (kc.astype(F32) * to_end).astype(BF16)
            kv = lax.dot_general(kd, vc, (((0,), (0,)), ((), ())),
                                 preferred_element_type=F32)
            R = kv if R is None else cd * R + kv
        ms = jnp.mean(o * o, axis=-1, keepdims=True)
        y = o * lax.rsqrt(ms + 1e-6) * gn_ref[...]
        y = y * g_ref[rows, :].astype(F32)
        o_ref[rows, :] = y.astype(o_ref.dtype)


def _retention(u, dmask, to_end, from_start, cdec, gn, B, S):
    C = RET_KCHUNK
    kern = functools.partial(_ret_kernel, n_chunks=S // C)
    col = lambda off: (lambda b, h: (b, off + h))
    hspec = lambda shp: pl.BlockSpec((None,) + shp, lambda b, h: (h, 0, 0))
    return pl.pallas_call(
        kern,
        grid=(B, H_R),
        in_specs=[
            pl.BlockSpec((S, DK_R), col(16)),
            pl.BlockSpec((S, DK_R), col(20)),
            pl.BlockSpec((S, DV_R), col(24)),
            pl.BlockSpec((S, DV_R), col(28)),
            hspec((C, C)),
            hspec((C, DK_R)),
            hspec((C, DK_R)),
            hspec((1, 1)),
            _resident((1, DV_R)),
        ],
        out_specs=pl.BlockSpec((S, DV_R), lambda b, h: (b, h)),
        out_shape=jax.ShapeDtypeStruct((B * S, BRANCH_W), BF16),
        compiler_params=_cparams(("arbitrary", "arbitrary")),
        name="retention",
    )(u, u, u, u, dmask, to_end, from_start, cdec, gn)


def _mix_ffn_kernel(x_ref, ya_ref, yb_ref, yc_ref, g0_ref, g1_ref, g2_ref,
                    wb_ref, wo_ref, n2_ref, wfi_ref, wfo_ref, o_ref, act_ref):
    y = g0_ref[...].astype(F32) * jnp.dot(ya_ref[...], wb_ref[0],
                                          preferred_element_type=F32)
    y = y + g1_ref[...].astype(F32) * jnp.dot(yb_ref[...], wb_ref[1],
                                              preferred_element_type=F32)
    y = y + g2_ref[...].astype(F32) * jnp.dot(yc_ref[...], wb_ref[2],
                                              preferred_element_type=F32)
    x1 = x_ref[...] + jnp.dot(y.astype(BF16), wo_ref[...],
                              preferred_element_type=F32)
    ms = jnp.mean(x1 * x1, axis=-1, keepdims=True)
    h2 = (x1 * lax.rsqrt(ms + 1e-6) * n2_ref[...]).astype(BF16)
    for c in range(D_FF // FF_CHUNK + (1 if D_FF % FF_CHUNK else 0)):
        lo = c * FF_CHUNK
        hi = min(D_FF, lo + FF_CHUNK)
        g = jnp.dot(h2, wfi_ref[:, lo:hi], preferred_element_type=F32)
        up = jnp.dot(h2, wfi_ref[:, D_FF + lo:D_FF + hi],
                     preferred_element_type=F32)
        act_ref[:, lo:hi] = (g * _sigmoid(g) * up).astype(BF16)
    o_ref[...] = x1 + jnp.dot(act_ref[...], wfo_ref[...],
                              preferred_element_type=F32)


def _mix_ffn(x2, ya, yb, yc, u, wb_all, wo_all, n2, wfi_all, wfo_all, l):
    T = x2.shape[0]
    tm = TM_PROJ
    gate_blk = U_GATE_OFF // D_MODEL
    tok = lambda w: pl.BlockSpec((tm, w), lambda i: (i, 0))
    gate = lambda n: pl.BlockSpec((tm, D_MODEL), lambda i: (i, gate_blk + n))
    return pl.pallas_call(
        _mix_ffn_kernel,
        grid=(T // tm,),
        in_specs=[
            tok(D_MODEL), tok(BRANCH_W), tok(BRANCH_W), tok(BRANCH_W),
            gate(0), gate(1), gate(2),
            _layer_resident((N_BRANCH, BRANCH_W, D_MODEL), l),
            _layer_resident((D_MODEL, D_MODEL), l),
            _resident((1, D_MODEL)),
            _layer_resident((D_MODEL, 2 * D_FF), l),
            _layer_resident((D_FF, D_MODEL), l),
        ],
        out_specs=tok(D_MODEL),
        out_shape=jax.ShapeDtypeStruct((T, D_MODEL), F32),
        scratch_shapes=[pltpu.VMEM((tm, D_FF), BF16)],
        compiler_params=_cparams(("arbitrary",)),
        name="mix_ffn",
    )(x2, ya, yb, yc, u, u, u, wb_all, wo_all, n2, wfi_all, wfo_all)


def _retention_tables():
    C = RET_KCHUNK
    log_g = jnp.log(1.0 - 2.0 ** (-5.0 - jnp.arange(H_R, dtype=F32)))
    idx = jnp.arange(C)
    diff = idx[:, None] - idx[None, :]
    scale = DK_R ** -0.5
    dmask = jnp.where(diff >= 0,
                      jnp.exp(jnp.maximum(diff, 0).astype(F32)[None]
                              * log_g[:, None, None]), 0.0) * scale
    to_end = jnp.exp((C - 1 - idx).astype(F32)[None, :, None]
                     * log_g[:, None, None]) * scale
    from_start = jnp.exp((idx + 1).astype(F32)[None, :, None]
                         * log_g[:, None, None])
    cdec = jnp.exp(C * log_g)[:, None, None]
    lanes = (H_R, C, DK_R)
    return (dmask, jnp.broadcast_to(to_end, lanes),
            jnp.broadcast_to(from_start, lanes), cdec)


def kernel(x, norm1_g, w_in, q_norm_g, k_norm_g, diff_lambda, attn_sub_g,
           conv_w, conv_b, conv_ln_g, conv_ln_b, ret_gn_g, b_gate, w_branch,
           w_out, norm2_g, w_ffn_in, w_ffn_out):
    B, S, D = x.shape
    depth = w_in.shape[0]
    assert D == D_MODEL and S % BQ_B == 0 and S % TC_CONV == 0
    assert S % RET_KCHUNK == 0 and (B * S) % TM_PROJ == 0

    grp = jnp.arange(SEG) // DK_A
    gsum = (grp[:, None] == grp[None, :]).astype(BF16)
    dmask, to_end, from_start, cdec = _retention_tables()
    reps = SEG // DK_A

    w_in_b = w_in.astype(BF16)
    w_branch_b = w_branch.astype(BF16)
    w_out_b = w_out.astype(BF16)
    w_ffn_in_b = w_ffn_in.astype(BF16)
    w_ffn_out_b = w_ffn_out.astype(BF16)

    x2 = x.reshape(B * S, D)
    for l in range(depth):
        lam_init = 0.8 - 0.6 * math.exp(-0.3 * l)
        qg = (jnp.tile(q_norm_g[l].astype(F32), reps) * (DK_A ** -0.5))[None, :]
        kg = jnp.tile(k_norm_g[l].astype(F32), reps)[None, :]
        u = _inproj(x2, norm1_g[l][None, :], w_in_b, l, qg, kg, gsum,
                    b_gate[l][None, :])
        bound = (math.sqrt(DK_A) * jnp.max(jnp.abs(q_norm_g[l].astype(F32)))
                 * jnp.max(jnp.abs(k_norm_g[l].astype(F32))))
        dl = diff_lambda[l].astype(F32)
        sg = attn_sub_g[l].astype(F32)[None, :]

        def attn_bounded(u=u, bound=bound, dl=dl, sg=sg, lam_init=lam_init):
            return _attention_bounded(u, bound.reshape(1, 1), dl, sg, lam_init,
                                      B, S)

        def attn_online(u=u, dl=dl, sg=sg, lam_init=lam_init):
            return _attention(u, dl, sg, lam_init, B, S)

        ya = lax.cond(bound <= ATTN_BOUND_MAX, attn_bounded, attn_online)
        yb = _conv(u, conv_w[l], conv_b[l][None, :], conv_ln_g[l][None, :],
                   conv_ln_b[l][None, :], B, S)
        yc = _retention(u, dmask, to_end, from_start, cdec,
                        ret_gn_g[l][None, :], B, S)
        x2 = _mix_ffn(x2, ya, yb, yc, u, w_branch_b, w_out_b,
                      norm2_g[l][None, :], w_ffn_in_b, w_ffn_out_b, l)
    return x2.reshape(B, S, D)
```
